```python
import jax, jax.numpy as jnp
from jax import lax
import numpy as np

D_MODEL = 1024
BATCH = 8
SEQ = 2048
DEPTH = 2
DEC_BATCH = 16
DEC_SEQ = 2048
PAST_LEN = 128

HEAD_DIM = 64
A_HEADS = D_MODEL // 2 // HEAD_DIM
A_KV_HEADS = A_HEADS // 4
A_GROUP = A_HEADS // A_KV_HEADS
A_HALF_WINDOW = 128
B_HEADS = D_MODEL // 2 // HEAD_DIM
B_PATTERNS = ((128, 1), (512, 4), (2048, 16))
A_WIDTH = A_HEADS * HEAD_DIM
AKV_WIDTH = A_KV_HEADS * HEAD_DIM
B_WIDTH = B_HEADS * HEAD_DIM
MIX_WIDTH = A_WIDTH + B_WIDTH
QKV_COLS = A_WIDTH + 2 * AKV_WIDTH + 3 * B_WIDTH
SPLITS = [A_WIDTH, A_WIDTH + AKV_WIDTH, A_WIDTH + 2 * AKV_WIDTH,
          A_WIDTH + 2 * AKV_WIDTH + B_WIDTH, A_WIDTH + 2 * AKV_WIDTH + 2 * B_WIDTH]
N_GROUPS = 4
EXPERTS_PER_GROUP = 8
N_EXPERTS = N_GROUPS * EXPERTS_PER_GROUP
TOP_K_FINE = 2
D_EXPERT = D_MODEL // 2
DEEPNORM_ALPHA = (2 * DEPTH) ** 0.25
DEEPNORM_BETA = (8 * DEPTH) ** -0.25
LN_EPS = 1e-5
NEG_INF = -1e30

kernel_name = "hymba_window_dilated_hmoe_deepnorm"


def alibi_slopes(n):
    return jnp.asarray(np.array([2.0 ** (-8.0 * (h + 1) / n) for h in range(n)], dtype=np.float32))


def layernorm(x, g, b):
    xf = x.astype(jnp.float32)
    mu = xf.mean(-1, keepdims=True)
    var = jnp.square(xf - mu).mean(-1, keepdims=True)
    y = (xf - mu) * lax.rsqrt(var + LN_EPS) * g.astype(jnp.float32) + b.astype(jnp.float32)
    return y.astype(x.dtype)


def rmsnorm(x, g):
    xf = x.astype(jnp.float32)
    return xf * lax.rsqrt(jnp.square(xf).mean(-1, keepdims=True) + LN_EPS) * g.astype(jnp.float32)


def banded_attention(q, k, v, half, dist_unit, slopes, sink):
    n_b, L, hkv, g, dh = q.shape
    blk = half
    nblk = -(-L // blk)
    pad = nblk * blk - L
    if pad:
        q = jnp.pad(q, ((0, 0), (0, pad), (0, 0), (0, 0), (0, 0)))
        k = jnp.pad(k, ((0, 0), (0, pad), (0, 0), (0, 0)))
        v = jnp.pad(v, ((0, 0), (0, pad), (0, 0), (0, 0)))
    qb = q.reshape(n_b, nblk, blk, hkv, g, dh)

    def neighbours(t):
        tb = jnp.pad(t.reshape(n_b, nblk, blk, hkv, dh), ((0, 0), (1, 1), (0, 0), (0, 0), (0, 0)))
        return jnp.concatenate([tb[:, :-2], tb[:, 1:-1], tb[:, 2:]], axis=2)

    kb, vb = neighbours(k), neighbours(v)
    qi = jnp.arange(blk)
    kj = jnp.arange(3 * blk)
    rel = kj[None, :] - blk - qi[:, None]
    dist = jnp.abs(rel)
    kpos = jnp.arange(nblk)[:, None] * blk - blk + kj[None, :]
    valid = (dist <= half)[None] & ((kpos >= 0) & (kpos < L))[:, None, :]
    s = jnp.einsum('nbqhgd,nbkhd->nbhgqk', qb, kb, preferred_element_type=jnp.float32) * (dh ** -0.5)
    s = s - slopes.astype(jnp.float32)[:, :, None, None] * (dist.astype(jnp.float32) * dist_unit)
    s = jnp.where(valid[:, None, None], s, NEG_INF)
    m = s.max(-1)
    if sink is not None:
        sk = sink.astype(jnp.float32)[:, :, None]
        m = jnp.maximum(m, sk)
    p = jnp.exp(s - m[..., None])
    denom = p.sum(-1)
    if sink is not None:
        denom = denom + jnp.exp(sk - m)
    o = jnp.einsum('nbhgqk,nbkhd->nbqhgd', p, vb.astype(jnp.float32))
    o = o / jnp.moveaxis(denom, -1, 2)[..., None]
    lse = jnp.moveaxis(m + jnp.log(denom), -1, 2)
    o = o.reshape(n_b, nblk * blk, hkv, g, dh)[:, :L]
    lse = lse.reshape(n_b, nblk * blk, hkv, g)[:, :L]
    return o, lse


def dilated_attention(q, k, v, slopes):
    bn, S, H, dh = q.shape
    outs, lses = [], []
    for window, dil in B_PATTERNS:
        L = S // dil
        half = (window // 2) // dil

        def strided(t):
            return t.reshape(bn, L, dil, H, dh).transpose(0, 2, 1, 3, 4).reshape(bn * dil, L, H, dh)

        o, lse = banded_attention(strided(q)[:, :, :, None], strided(k), strided(v),
                                  half, float(dil), slopes[:, None], None)
        outs.append(o.reshape(bn, dil, L, H, dh).transpose(0, 2, 1, 3, 4).reshape(bn, S, H, dh))
        lses.append(lse.reshape(bn, dil, L, H).transpose(0, 2, 1, 3).reshape(bn, S, H))
    w = jax.nn.softmax(jnp.stack(lses), axis=0)
    return jnp.einsum('pbsh,pbshd->bshd', w, jnp.stack(outs))


def token_mixer(x, w_in, sink, g_mix, w_out):
    bn, S, _ = x.shape
    proj = jnp.einsum('bsd,dc->bsc', x, w_in)
    qa, ka, va, qb, kb, vb = jnp.split(proj, SPLITS, axis=-1)
    oa, _ = banded_attention(qa.reshape(bn, S, A_KV_HEADS, A_GROUP, HEAD_DIM),
                             ka.reshape(bn, S, A_KV_HEADS, HEAD_DIM),
                             va.reshape(bn, S, A_KV_HEADS, HEAD_DIM),
                             A_HALF_WINDOW, 1.0,
                             alibi_slopes(A_HEADS).reshape(A_KV_HEADS, A_GROUP),
                             sink.reshape(A_KV_HEADS, A_GROUP))
    ob = dilated_attention(qb.reshape(bn, S, B_HEADS, HEAD_DIM), kb.reshape(bn, S, B_HEADS, HEAD_DIM),
                           vb.reshape(bn, S, B_HEADS, HEAD_DIM), alibi_slopes(B_HEADS))
    y = jnp.concatenate([rmsnorm(oa.reshape(bn, S, A_WIDTH), g_mix[:A_WIDTH]),
                         rmsnorm(ob.reshape(bn, S, B_WIDTH), g_mix[A_WIDTH:])], axis=-1).astype(x.dtype)
    return jnp.einsum('bsc,cd->bsd', y, w_out)


def hier_moe(x, w_coarse, b_coarse, w_fine, b_fine, w_gate, w_up, w_down):
    def per_seq(xs):
        S = xs.shape[0]
        xf = xs.astype(jnp.float32)
        cl = xf @ w_coarse.astype(jnp.float32) + b_coarse.astype(jnp.float32)
        cp = jax.nn.softmax(cl, axis=-1)
        g_idx = jnp.argmax(cl, axis=-1)
        g_oh = jax.nn.one_hot(g_idx, N_GROUPS, dtype=jnp.float32)
        g_w = jnp.sum(cp * g_oh, axis=-1)
        fl = (xf @ w_fine.astype(jnp.float32) + b_fine.astype(jnp.float32)).reshape(S, N_GROUPS, EXPERTS_PER_GROUP)
        fl_sel = jnp.einsum('sge,sg->se', fl, g_oh)
        top_v, top_i = lax.top_k(fl_sel, TOP_K_FINE)
        top_w = jax.nn.softmax(top_v, axis=-1) * g_w[:, None]
        eid = g_idx[:, None] * EXPERTS_PER_GROUP + top_i
        combine = jnp.sum(jax.nn.one_hot(eid, N_EXPERTS, dtype=jnp.float32) * top_w[..., None], axis=1)
        h = jax.nn.silu(jnp.einsum('sd,edf->sef', xs, w_gate)) * jnp.einsum('sd,edf->sef', xs, w_up)
        h = h * combine[..., None].astype(h.dtype)
        return jnp.einsum('sef,efd->sd', h, w_down).astype(xs.dtype)
    return lax.map(per_seq, x)


def trunk(x, w_in, attn_sink, g_mix, w_out, ln1_g, ln1_b, w_coarse, b_coarse, w_fine, b_fine,
          w_gate, w_up, w_down, ln2_g, ln2_b):
    for l in range(DEPTH):
        x = layernorm(DEEPNORM_ALPHA * x + token_mixer(x, w_in[l], attn_sink[l], g_mix[l], w_out[l]),
                      ln1_g[l], ln1_b[l])
        x = layernorm(DEEPNORM_ALPHA * x + hier_moe(x, w_coarse[l], b_coarse[l], w_fine[l], b_fine[l],
                                                   w_gate[l], w_up[l], w_down[l]),
                      ln2_g[l], ln2_b[l])
    return x


def setup_inputs(seed: int = 0) -> dict:
    key = jax.random.key(seed)
    ks = jax.random.split(key, 17)
    f32 = jnp.float32
    nrm = lambda k, shape, scale: jax.random.normal(k, shape, f32) * scale
    return {
        "x_prompt": nrm(ks[0], (BATCH, SEQ, D_MODEL), 1.0),
        "x_sample": nrm(ks[1], (DEC_BATCH, DEC_SEQ, D_MODEL), 1.0),
        "w_in": nrm(ks[2], (DEPTH, D_MODEL, QKV_COLS), D_MODEL ** -0.5),
        "attn_sink": nrm(ks[3], (DEPTH, A_HEADS), 0.5),
        "g_mix": 1.0 + nrm(ks[4], (DEPTH, MIX_WIDTH), 0.02),
        "w_out": nrm(ks[5], (DEPTH, MIX_WIDTH, D_MODEL), MIX_WIDTH ** -0.5 * DEEPNORM_BETA),
        "ln1_g": 1.0 + nrm(ks[6], (DEPTH, D_MODEL), 0.02),
        "ln1_b": nrm(ks[7], (DEPTH, D_MODEL), 0.02),
        "w_coarse": nrm(ks[8], (DEPTH, D_MODEL, N_GROUPS), D_MODEL ** -0.5),
        "b_coarse": nrm(ks[9], (DEPTH, N_GROUPS), 0.01),
        "w_fine": nrm(ks[10], (DEPTH, D_MODEL, N_EXPERTS), D_MODEL ** -0.5),
        "b_fine": nrm(ks[11], (DEPTH, N_EXPERTS), 0.01),
        "w_gate": nrm(ks[12], (DEPTH, N_EXPERTS, D_MODEL, D_EXPERT), D_MODEL ** -0.5),
        "w_up": nrm(ks[13], (DEPTH, N_EXPERTS, D_MODEL, D_EXPERT), D_MODEL ** -0.5),
        "w_down": nrm(ks[14], (DEPTH, N_EXPERTS, D_EXPERT, D_MODEL), D_EXPERT ** -0.5 * DEEPNORM_BETA),
        "ln2_g": 1.0 + nrm(ks[15], (DEPTH, D_MODEL), 0.02),
        "ln2_b": nrm(ks[16], (DEPTH, D_MODEL), 0.02),
    }


def reference(x_prompt, x_sample, w_in, attn_sink, g_mix, w_out, ln1_g, ln1_b, w_coarse, b_coarse,
              w_fine, b_fine, w_gate, w_up, w_down, ln2_g, ln2_b):
    y_prompt = trunk(x_prompt, w_in, attn_sink, g_mix, w_out, ln1_g, ln1_b, w_coarse, b_coarse,
                     w_fine, b_fine, w_gate, w_up, w_down, ln2_g, ln2_b)
    y_sample = trunk(x_sample, w_in, attn_sink, g_mix, w_out, ln1_g, ln1_b, w_coarse, b_coarse,
                     w_fine, b_fine, w_gate, w_up, w_down, ln2_g, ln2_b)
    return (y_prompt, y_sample)
```

```python
import jax
import jax.numpy as jnp
import numpy as np
from jax import lax
from jax.experimental import pallas as pl
from jax.experimental.pallas import tpu as pltpu

D_MODEL = 1024
SEQ = 2048
DEPTH = 2
HEAD_DIM = 64
A_HEADS = 8
A_KV_HEADS = 2
A_HALF_WINDOW = 128
B_HEADS = 8
B_DILATIONS = (1, 4, 16)
B_HALF = 64
A_WIDTH = A_HEADS * HEAD_DIM
AKV_WIDTH = A_KV_HEADS * HEAD_DIM
B_WIDTH = B_HEADS * HEAD_DIM
N_GROUPS = 4
EXPERTS_PER_GROUP = 8
N_EXPERTS = N_GROUPS * EXPERTS_PER_GROUP
D_EXPERT = D_MODEL // 2
DEEPNORM_ALPHA = (2 * DEPTH) ** 0.25
LN_EPS = 1e-5
MASK_DIST = 1e30

LANES = 128
SUBLANES = 8
Q_BLOCK = 128
VMEM_LIMIT = 56 * 1024 * 1024


def _alibi_slopes(n):
    return jnp.asarray(np.array([2.0 ** (-8.0 * (h + 1) / n) for h in range(n)], dtype=np.float32))


def _params(n_axes):
    return pltpu.CompilerParams(dimension_semantics=("arbitrary",) * n_axes, vmem_limit_bytes=VMEM_LIMIT)


PROJ_TM = 512
W_CAT_COLS = 2560


def _inproj_kernel(x_ref, w_ref, qa_ref, ka_ref, va_ref,
                   q1_ref, k1_ref, v1_ref, q4_ref, k4_ref, v4_ref, q16_ref, k16_ref, v16_ref,
                   acc_ref):
    xb = x_ref[...].astype(jnp.bfloat16)

    def proj(c0, width):
        return jnp.dot(xb, w_ref[:, c0:c0 + width], preferred_element_type=jnp.float32)

    qa_ref[...] = proj(0, 512).astype(jnp.bfloat16)
    ka_ref[...] = proj(512, 256).astype(jnp.bfloat16)
    va_ref[...] = proj(768, 256).astype(jnp.bfloat16)
    for j, (o1, o4, o16) in enumerate(((q1_ref, q4_ref, q16_ref), (k1_ref, k4_ref, k16_ref),
                                       (v1_ref, v4_ref, v16_ref))):
        acc = proj(1024 + 512 * j, 512)
        o1[0] = acc.astype(jnp.bfloat16)
        for c in range(512 // LANES):
            acc_ref[c] = acc[:, c * LANES:(c + 1) * LANES]
        for c in range(512 // LANES):
            lanes = slice(c * LANES, (c + 1) * LANES)
            for r in range(4):
                o4[r, :, lanes] = acc_ref[c, pl.ds(r, PROJ_TM // 4, stride=4), :].astype(jnp.bfloat16)
            for r in range(16):
                o16[r, :, lanes] = acc_ref[c, pl.ds(r, PROJ_TM // 16, stride=16), :].astype(jnp.bfloat16)


def _inproj(x, w_cat):
    bt = x.shape[0]
    nt = SEQ // PROJ_TM
    bf = jnp.bfloat16

    def dil_shape(d):
        return jax.ShapeDtypeStruct((bt, d, SEQ // d, B_WIDTH), bf)

    def dil_spec(d):
        return pl.BlockSpec((None, d, PROJ_TM // d, B_WIDTH), lambda b, i: (b, 0, i, 0))

    out_shape = [jax.ShapeDtypeStruct((bt, SEQ, 512), bf),
                 jax.ShapeDtypeStruct((bt, SEQ, 256), bf),
                 jax.ShapeDtypeStruct((bt, SEQ, 256), bf)]
    out_specs = [pl.BlockSpec((None, PROJ_TM, 512), lambda b, i: (b, i, 0)),
                 pl.BlockSpec((None, PROJ_TM, 256), lambda b, i: (b, i, 0)),
                 pl.BlockSpec((None, PROJ_TM, 256), lambda b, i: (b, i, 0))]
    for d in B_DILATIONS:
        out_shape += [dil_shape(d)] * 3
        out_specs += [dil_spec(d)] * 3
    return pl.pallas_call(
        _inproj_kernel,
        grid=(bt, nt),
        in_specs=[pl.BlockSpec((None, PROJ_TM, D_MODEL), lambda b, i: (b, i, 0)),
                  pl.BlockSpec((D_MODEL, W_CAT_COLS), lambda b, i: (0, 0))],
        out_specs=out_specs,
        out_shape=out_shape,
        scratch_shapes=[pltpu.VMEM((512 // LANES, PROJ_TM, LANES), jnp.float32)],
        compiler_params=_params(2),
        name="inproj",
    )(x, w_cat)


BLOCKS_PER_ITER = 2


def _dist_tables(nk, half):
    r = np.arange(Q_BLOCK)[:, None]
    c = np.arange(nk)[None, :]
    tabs = []
    for rel in (half, 0, nk - Q_BLOCK):
        d = np.abs(rel + r - c).astype(np.float32)
        tabs.append(np.where(d <= half, d, np.float32(MASK_DIST)))
    return np.stack(tabs).astype(np.float32)


def _fill_bias(bias_ref, dist_ref, slope_a, slope_b, nk):
    for g in range(dist_ref.shape[0]):
        d = dist_ref[g]
        bias_ref[g, :, 0:nk] = d * (-slope_a)
        bias_ref[g, :, nk:2 * nk] = d * (-slope_b)


def _geometry(blk, nblk):
    return jnp.where(blk == 0, 1, jnp.where(blk == nblk - 1, 2, 0))


def _band_block(q2, kw, vw, bias, nk, sink_a=None, sink_b=None):
    lane_k = lax.broadcasted_iota(jnp.int32, (nk, LANES), 1) < HEAD_DIM
    zero = jnp.zeros_like(kw)
    kbd = jnp.concatenate([jnp.where(lane_k, kw, zero), jnp.where(lane_k, zero, kw)], axis=0)
    vbd = jnp.concatenate([jnp.where(lane_k, vw, zero), jnp.where(lane_k, zero, vw)], axis=0)
    s = lax.dot_general(q2, kbd, (((1,), (1,)), ((), ())), preferred_element_type=jnp.float32) + bias
    s_a, s_b = s[:, :nk], s[:, nk:]
    m_a = jnp.max(s_a, axis=-1, keepdims=True)
    m_b = jnp.max(s_b, axis=-1, keepdims=True)
    if sink_a is not None:
        m_a = jnp.maximum(m_a, sink_a)
        m_b = jnp.maximum(m_b, sink_b)
    p_a = jnp.exp(s_a - m_a)
    p_b = jnp.exp(s_b - m_b)
    l_a = jnp.sum(p_a, axis=-1, keepdims=True)
    l_b = jnp.sum(p_b, axis=-1, keepdims=True)
    if sink_a is not None:
        l_a = l_a + jnp.exp(sink_a - m_a)
        l_b = l_b + jnp.exp(sink_b - m_b)
    p = jnp.concatenate([p_a, p_b], axis=1).astype(jnp.bfloat16)
    o = jnp.dot(p, vbd, preferred_element_type=jnp.float32)
    lane_o = lax.broadcasted_iota(jnp.int32, (Q_BLOCK, LANES), 1) < HEAD_DIM
    o = o / jnp.where(lane_o, l_a, l_b)
    lse = jnp.where(lane_o, m_a + jnp.log(l_a), m_b + jnp.log(l_b))
    return o, lse


def _attn_a_kernel(slopes_ref, sink_ref, dist_ref, q_ref, k_ref, v_ref, o_ref, bias_ref):
    hp = pl.program_id(1)
    sink_a = sink_ref[2 * hp]
    sink_b = sink_ref[2 * hp + 1]
    nk = 3 * A_HALF_WINDOW
    nblk = SEQ // Q_BLOCK
    _fill_bias(bias_ref, dist_ref, slopes_ref[2 * hp], slopes_ref[2 * hp + 1], nk)

    def body(i, carry):
        for u in range(BLOCKS_PER_ITER):
            blk = i * BLOCKS_PER_ITER + u
            q0 = pl.multiple_of(blk * Q_BLOCK, Q_BLOCK)
            ws = pl.multiple_of(jnp.clip(q0 - A_HALF_WINDOW, 0, SEQ - nk), Q_BLOCK)
            o, _ = _band_block(q_ref[pl.ds(q0, Q_BLOCK), :], k_ref[pl.ds(ws, nk), :], v_ref[pl.ds(ws, nk), :],
                               bias_ref[_geometry(blk, nblk)], nk, sink_a, sink_b)
            o_ref[pl.ds(q0, Q_BLOCK), :] = o
        return carry

    lax.fori_loop(0, nblk // BLOCKS_PER_ITER, body, 0)


def _attn_a(qa, ka2, va2, slopes, sink):
    bt = qa.shape[0]
    nk = 3 * A_HALF_WINDOW
    dist = jnp.asarray(_dist_tables(nk, A_HALF_WINDOW))
    return pl.pallas_call(
        _attn_a_kernel,
        grid_spec=pltpu.PrefetchScalarGridSpec(
            num_scalar_prefetch=2,
            grid=(bt, A_HEADS // 2),
            in_specs=[pl.BlockSpec((3, Q_BLOCK, nk), lambda b, h, *_: (0, 0, 0)),
                      pl.BlockSpec((None, SEQ, LANES), lambda b, h, *_: (b, 0, h)),
                      pl.BlockSpec((None, SEQ, LANES), lambda b, h, *_: (b, 0, h // 2)),
                      pl.BlockSpec((None, SEQ, LANES), lambda b, h, *_: (b, 0, h // 2))],
            out_specs=pl.BlockSpec((None, SEQ, LANES), lambda b, h, *_: (b, 0, h)),
            scratch_shapes=[pltpu.VMEM((3, Q_BLOCK, 2 * nk), jnp.float32)],
        ),
        out_shape=jax.ShapeDtypeStruct((bt, SEQ, A_WIDTH), jnp.float32),
        compiler_params=_params(2),
        name="attn_window",
    )(slopes, sink, dist, qa, ka2, va2)


def _attn_b_kernel(slopes_ref, dist_ref, dist16_ref,
                   q1_ref, k1_ref, v1_ref, q4_ref, k4_ref, v4_ref, q16_ref, k16_ref, v16_ref,
                   o_ref, bias1_ref, bias4_ref, bias16_ref, osc_ref, lsc_ref):
    hp = pl.program_id(1)
    refs = ((q1_ref, k1_ref, v1_ref, dist_ref, bias1_ref), (q4_ref, k4_ref, v4_ref, dist_ref, bias4_ref),
            (q16_ref, k16_ref, v16_ref, dist16_ref, bias16_ref))
    for p, dil in enumerate(B_DILATIONS):
        q_ref, k_ref, v_ref, d_ref, bias_ref = refs[p]
        length = SEQ // dil
        nblk = length // Q_BLOCK
        nk = min(4 * B_HALF, length)
        _fill_bias(bias_ref, d_ref, slopes_ref[2 * hp] * float(dil), slopes_ref[2 * hp + 1] * float(dil), nk)

        def body(i, carry, q_ref=q_ref, k_ref=k_ref, v_ref=v_ref, bias_ref=bias_ref, length=length,
                 nblk=nblk, nk=nk, dil=dil, p=p):
            for u in range(BLOCKS_PER_ITER):
                idx = i * BLOCKS_PER_ITER + u
                r = idx // nblk
                blk = idx % nblk
                q0 = pl.multiple_of(blk * Q_BLOCK, Q_BLOCK)
                ws = pl.multiple_of(jnp.clip(q0 - B_HALF, 0, length - nk), B_HALF)
                bias = bias_ref[0] if nblk == 1 else bias_ref[_geometry(blk, nblk)]
                o, lse = _band_block(q_ref[r, pl.ds(q0, Q_BLOCK), :], k_ref[r, pl.ds(ws, nk), :],
                                     v_ref[r, pl.ds(ws, nk), :], bias, nk)
                if dil == 1:
                    rows = pl.ds(q0, Q_BLOCK)
                else:
                    rows = pl.ds(q0 * dil + r, Q_BLOCK, stride=dil)
                osc_ref[p, rows, :] = o
                lsc_ref[p, rows, :] = lse
            return carry

        lax.fori_loop(0, SEQ // Q_BLOCK // BLOCKS_PER_ITER, body, 0)

    def combine(i, carry):
        rows = pl.ds(pl.multiple_of(i * Q_BLOCK, Q_BLOCK), Q_BLOCK)
        l0, l1, l2 = lsc_ref[0, rows, :], lsc_ref[1, rows, :], lsc_ref[2, rows, :]
        m = jnp.maximum(jnp.maximum(l0, l1), l2)
        e0, e1, e2 = jnp.exp(l0 - m), jnp.exp(l1 - m), jnp.exp(l2 - m)
        num = e0 * osc_ref[0, rows, :] + e1 * osc_ref[1, rows, :] + e2 * osc_ref[2, rows, :]
        o_ref[rows, :] = num / (e0 + e1 + e2)
        return carry

    lax.fori_loop(0, SEQ // Q_BLOCK, combine, 0)


def _attn_b(qkv_layouts, slopes):
    bt = qkv_layouts[0].shape[0]
    nk = 4 * B_HALF
    dist = jnp.asarray(_dist_tables(nk, B_HALF))
    dist16 = jnp.asarray(_dist_tables(Q_BLOCK, B_HALF)[1:2])
    in_specs = [pl.BlockSpec((3, Q_BLOCK, nk), lambda b, h, *_: (0, 0, 0)),
                pl.BlockSpec((1, Q_BLOCK, Q_BLOCK), lambda b, h, *_: (0, 0, 0))]
    for d in B_DILATIONS:
        in_specs += [pl.BlockSpec((None, d, SEQ // d, LANES), lambda b, h, *_: (b, 0, 0, h))] * 3
    return pl.pallas_call(
        _attn_b_kernel,
        grid_spec=pltpu.PrefetchScalarGridSpec(
            num_scalar_prefetch=1,
            grid=(bt, B_HEADS // 2),
            in_specs=in_specs,
            out_specs=pl.BlockSpec((None, SEQ, LANES), lambda b, h, *_: (b, 0, h)),
            scratch_shapes=[pltpu.VMEM((3, Q_BLOCK, 2 * nk), jnp.float32),
                            pltpu.VMEM((3, Q_BLOCK, 2 * nk), jnp.float32),
                            pltpu.VMEM((1, Q_BLOCK, 2 * Q_BLOCK), jnp.float32),
                            pltpu.VMEM((3, SEQ, LANES), jnp.float32),
                            pltpu.VMEM((3, SEQ, LANES), jnp.float32)],
        ),
        out_shape=jax.ShapeDtypeStruct((bt, SEQ, B_WIDTH), jnp.float32),
        compiler_params=_params(2),
        name="attn_dilated",
    )(slopes, dist, dist16, *qkv_layouts)


def _prep_w_in(w_in_l):
    scale = HEAD_DIM ** -0.5
    qa = w_in_l[:, :A_WIDTH] * scale
    ka = w_in_l[:, A_WIDTH:A_WIDTH + AKV_WIDTH]
    va = w_in_l[:, A_WIDTH + AKV_WIDTH:A_WIDTH + 2 * AKV_WIDTH]
    rest = w_in_l[:, A_WIDTH + 2 * AKV_WIDTH:]
    qb = rest[:, :B_WIDTH] * scale
    kvb = rest[:, B_WIDTH:]

    def dup(w):
        h0, h1 = w[:, :HEAD_DIM], w[:, HEAD_DIM:]
        return jnp.concatenate([h0, h0, h1, h1], axis=1)

    return jnp.concatenate([qa, dup(ka), dup(va), qb, kvb], axis=1).astype(jnp.bfloat16)


def _token_mixer_heads(x, w_cat, sink_l):
    outs = _inproj(x, w_cat)
    qa, ka2, va2 = outs[:3]
    oa = _attn_a(qa, ka2, va2, _alibi_slopes(A_HEADS), sink_l)
    ob = _attn_b(outs[3:], _alibi_slopes(B_HEADS))
    return oa, ob


MIX_TM = 512
ROUTE_LANES = LANES
FINE_LANE0 = N_GROUPS


def _layernorm(h, g, b):
    mu = jnp.mean(h, axis=-1, keepdims=True)
    c = h - mu
    var = jnp.mean(c * c, axis=-1, keepdims=True)
    return c * lax.rsqrt(var + LN_EPS) * g + b


def _rmsnorm(o, g):
    return o * lax.rsqrt(jnp.mean(o * o, axis=-1, keepdims=True) + LN_EPS) * g


def _mix_kernel(x_ref, oa_ref, ob_ref, gmix_ref, wout_ref, lng_ref, lnb_ref, wr_ref, br_ref,
                x1_ref, meta_ref, metat_ref):
    ya = _rmsnorm(oa_ref[...], gmix_ref[:, :A_WIDTH])
    yb = _rmsnorm(ob_ref[...], gmix_ref[:, A_WIDTH:])
    y = jnp.concatenate([ya, yb], axis=-1).astype(jnp.bfloat16)
    att = jnp.dot(y, wout_ref[...], preferred_element_type=jnp.float32)
    x1 = _layernorm(DEEPNORM_ALPHA * x_ref[...] + att, lng_ref[...], lnb_ref[...])
    x1_ref[...] = x1

    logits = jnp.dot(x1, wr_ref[...], preferred_element_type=jnp.float32,
                     precision=lax.Precision.HIGHEST) + br_ref[...]
    lane = lax.broadcasted_iota(jnp.int32, logits.shape, 1).astype(jnp.float32)
    neg = jnp.float32(-jnp.inf)
    big = jnp.float32(ROUTE_LANES)
    is_coarse = lane < N_GROUPS
    cl = jnp.where(is_coarse, logits, neg)
    cmax = jnp.max(cl, axis=-1, keepdims=True)
    g_idx = jnp.min(jnp.where(cl == cmax, lane, big), axis=-1, keepdims=True)
    g_w = 1.0 / jnp.sum(jnp.where(is_coarse, jnp.exp(logits - cmax), 0.0), axis=-1, keepdims=True)
    lo = FINE_LANE0 + EXPERTS_PER_GROUP * g_idx
    fl = jnp.where((lane >= lo) & (lane < lo + EXPERTS_PER_GROUP), logits, neg)
    v1 = jnp.max(fl, axis=-1, keepdims=True)
    i1 = jnp.min(jnp.where(fl == v1, lane, big), axis=-1, keepdims=True)
    fl2 = jnp.where(lane == i1, neg, fl)
    v2 = jnp.max(fl2, axis=-1, keepdims=True)
    i2 = jnp.min(jnp.where(fl2 == v2, lane, big), axis=-1, keepdims=True)
    t = jnp.exp(v2 - v1)
    w1 = g_w / (1.0 + t)
    w2 = g_w * t / (1.0 + t)
    meta = jnp.where(lane == 0, i1 - FINE_LANE0,
                     jnp.where(lane == 1, i2 - FINE_LANE0,
                               jnp.where(lane == 2, w1, jnp.where(lane == 3, w2, 0.0))))
    meta_ref[...] = meta
    metat_ref[...] = meta.T[:SUBLANES, :]


def _mix(x, oa, ob, gmix, wout_bf, lng, lnb, wr, br):
    t = x.shape[0]
    row = lambda i: (i, 0)
    const = lambda i: (0, 0)
    return pl.pallas_call(
        _mix_kernel,
        grid=(t // MIX_TM,),
        in_specs=[pl.BlockSpec((MIX_TM, D_MODEL), row),
                  pl.BlockSpec((MIX_TM, A_WIDTH), row),
                  pl.BlockSpec((MIX_TM, B_WIDTH), row),
                  pl.BlockSpec((1, D_MODEL), const),
                  pl.BlockSpec((D_MODEL, D_MODEL), const),
                  pl.BlockSpec((1, D_MODEL), const),
                  pl.BlockSpec((1, D_MODEL), const),
                  pl.BlockSpec((D_MODEL, ROUTE_LANES), const),
                  pl.BlockSpec((1, ROUTE_LANES), const)],
        out_specs=[pl.BlockSpec((MIX_TM, D_MODEL), row),
                   pl.BlockSpec((MIX_TM, ROUTE_LANES), row),
                   pl.BlockSpec((SUBLANES, MIX_TM), lambda i: (0, i))],
        out_shape=[jax.ShapeDtypeStruct((t, D_MODEL), jnp.float32),
                   jax.ShapeDtypeStruct((t, ROUTE_LANES), jnp.float32),
                   jax.ShapeDtypeStruct((SUBLANES, t), jnp.float32)],
        compiler_params=_params(1),
        name="mix_out_router",
    )(x, oa, ob, gmix, wout_bf, lng, lnb, wr, br)


PLAN_TM = 512


def _plan_kernel(metat_ref, tri_ref, route_ref, counts_ref, carry_ref):
    i = pl.program_id(0)

    @pl.when(i == 0)
    def _():
        carry_ref[...] = jnp.zeros_like(carry_ref)

    e1 = metat_ref[0:1, :]
    e2 = metat_ref[1:2, :]
    eid = lax.broadcasted_iota(jnp.int32, (N_EXPERTS, PLAN_TM), 0).astype(jnp.float32)
    oh1 = eid == e1
    oh2 = eid == e2
    cnt = jnp.where(oh1 | oh2, 1.0, 0.0)
    before = jnp.dot(cnt.astype(jnp.bfloat16), tri_ref[...], preferred_element_type=jnp.float32)
    tot = before + carry_ref[:, 0:1]
    rank1 = jnp.sum(jnp.where(oh1, tot, 0.0), axis=0, keepdims=True)
    rank2 = jnp.sum(jnp.where(oh2, tot, 0.0), axis=0, keepdims=True)
    carry_ref[...] = carry_ref[...] + jnp.sum(cnt, axis=1, keepdims=True)
    counts_ref[...] = carry_ref[...].astype(jnp.int32)
    sub = lax.broadcasted_iota(jnp.int32, (SUBLANES, PLAN_TM), 0)
    route = jnp.where(sub == 0, e1, jnp.where(sub == 1, e2, jnp.where(sub == 2, rank1,
                                                                      jnp.where(sub == 3, rank2, 0.0))))
    route_ref[...] = route.astype(jnp.int32)


def _plan(metat):
    t = metat.shape[1]
    tri = jnp.asarray(np.triu(np.ones((PLAN_TM, PLAN_TM), np.float32), k=1), jnp.bfloat16)
    return pl.pallas_call(
        _plan_kernel,
        grid=(t // PLAN_TM,),
        in_specs=[pl.BlockSpec((SUBLANES, PLAN_TM), lambda i: (0, i)),
                  pl.BlockSpec((PLAN_TM, PLAN_TM), lambda i: (0, 0))],
        out_specs=[pl.BlockSpec((SUBLANES, PLAN_TM), lambda i: (0, i)),
                   pl.BlockSpec((N_EXPERTS, LANES), lambda i: (0, 0))],
        out_shape=[jax.ShapeDtypeStruct((SUBLANES, t), jnp.int32),
                   jax.ShapeDtypeStruct((N_EXPERTS, LANES), jnp.int32)],
        scratch_shapes=[pltpu.VMEM((N_EXPERTS, LANES), jnp.float32)],
        compiler_params=_params(1),
        name="route_plan",
    )(metat, tri)


FFN_TM = 256
DISPATCH_TM = 512


def _sorted_rows(t):
    return 2 * t + N_EXPERTS * FFN_TM


def _dispatch_kernel(offs_ref, cnt_ref, route_ref, x_ref, xs_ref, zero_ref, sem, zsem):
    i = pl.program_id(0)

    def row_copy(tok, slot):
        return pltpu.make_async_copy(x_ref.at[pl.ds(tok, 1)], xs_ref.at[pl.ds(slot, 1)], sem)

    def issue(tok, carry):
        row_copy(tok, offs_ref[route_ref[0, tok]] + route_ref[2, tok]).start()
        row_copy(tok, offs_ref[route_ref[1, tok]] + route_ref[3, tok]).start()
        return carry

    lax.fori_loop(0, DISPATCH_TM, issue, 0)

    @pl.when(i == pl.num_programs(0) - 1)
    def _():
        zero_ref[...] = jnp.zeros_like(zero_ref)

        def pad_copy(slot):
            return pltpu.make_async_copy(zero_ref.at[pl.ds(0, 1)], xs_ref.at[pl.ds(slot, 1)], zsem)

        def per_expert(e, carry):
            n = cnt_ref[e]
            n_pad = ((n + FFN_TM - 1) // FFN_TM) * FFN_TM
            base = offs_ref[e]
            lax.fori_loop(n, n_pad, lambda r, c: (pad_copy(base + r).start(), c)[1], 0)
            lax.fori_loop(n, n_pad, lambda r, c: (pad_copy(base + r).wait(), c)[1], 0)
            return carry

        lax.fori_loop(0, N_EXPERTS, per_expert, 0)

    for _ in range(2):
        pltpu.make_async_copy(x_ref, xs_ref.at[pl.ds(0, DISPATCH_TM)], sem).wait()


def _dispatch(offs, counts, route, x1):
    t = x1.shape[0]
    return pl.pallas_call(
        _dispatch_kernel,
        grid_spec=pltpu.PrefetchScalarGridSpec(
            num_scalar_prefetch=2,
            grid=(t // DISPATCH_TM,),
            in_specs=[pl.BlockSpec((SUBLANES, DISPATCH_TM), lambda i, *_: (0, i), memory_space=pltpu.SMEM),
                      pl.BlockSpec((DISPATCH_TM, D_MODEL), lambda i, *_: (i, 0))],
            out_specs=pl.BlockSpec(memory_space=pl.ANY),
            scratch_shapes=[pltpu.VMEM((SUBLANES, D_MODEL), jnp.float32),
                            pltpu.SemaphoreType.DMA(()), pltpu.SemaphoreType.DMA(())],
        ),
        out_shape=jax.ShapeDtypeStruct((_sorted_rows(t), D_MODEL), jnp.float32),
        compiler_params=_params(1),
        name="moe_dispatch",
    )(offs, counts, route, x1)


def _ffn_kernel(te_ref, nused_ref, xs_ref, wg_ref, wu_ref, wd_ref, ys_ref):
    @pl.when(pl.program_id(0) < nused_ref[0])
    def _():
        x = xs_ref[...].astype(jnp.bfloat16)
        g = jnp.dot(x, wg_ref[...].astype(jnp.bfloat16), preferred_element_type=jnp.float32)
        u = jnp.dot(x, wu_ref[...].astype(jnp.bfloat16), preferred_element_type=jnp.float32)
        h = (g * jax.nn.sigmoid(g) * u).astype(jnp.bfloat16)
        ys_ref[...] = jnp.dot(h, wd_ref[...].astype(jnp.bfloat16), preferred_element_type=jnp.float32)


def _ffn(tile_expert, nused, xs, w_gate, w_up, w_down, layer):
    rows = xs.shape[0]
    tile = lambda j, te, nu: (jnp.minimum(j, nu[0] - 1), 0)
    wspec = lambda shape: pl.BlockSpec((None, None) + shape, lambda j, te, nu: (layer, te[j], 0, 0))
    return pl.pallas_call(
        _ffn_kernel,
        grid_spec=pltpu.PrefetchScalarGridSpec(
            num_scalar_prefetch=2,
            grid=(rows // FFN_TM,),
            in_specs=[pl.BlockSpec((FFN_TM, D_MODEL), tile),
                      wspec((D_MODEL, D_EXPERT)), wspec((D_MODEL, D_EXPERT)), wspec((D_EXPERT, D_MODEL))],
            out_specs=pl.BlockSpec((FFN_TM, D_MODEL), tile),
        ),
        out_shape=jax.ShapeDtypeStruct((rows, D_MODEL), jnp.float32),
        compiler_params=_params(1),
        name="moe_ffn",
    )(tile_expert, nused, xs, w_gate, w_up, w_down)


COMBINE_TM = 256


def _combine_kernel(offs_ref, route_ref, x1_ref, meta_ref, ys_ref, lng_ref, lnb_ref, o_ref, ybuf_ref, sem):
    def row_copy(slot, choice, tok):
        return pltpu.make_async_copy(ys_ref.at[pl.ds(slot, 1)], ybuf_ref.at[choice, pl.ds(tok, 1)], sem)

    def issue(tok, carry):
        row_copy(offs_ref[route_ref[0, tok]] + route_ref[2, tok], 0, tok).start()
        row_copy(offs_ref[route_ref[1, tok]] + route_ref[3, tok], 1, tok).start()
        return carry

    lax.fori_loop(0, COMBINE_TM, issue, 0)
    for c in range(2):
        pltpu.make_async_copy(ys_ref.at[pl.ds(0, COMBINE_TM)], ybuf_ref.at[c], sem).wait()
    meta = meta_ref[...]
    moe = meta[:, 2:3] * ybuf_ref[0] + meta[:, 3:4] * ybuf_ref[1]
    o_ref[...] = _layernorm(DEEPNORM_ALPHA * x1_ref[...] + moe, lng_ref[...], lnb_ref[...])


def _combine(offs, route, x1, meta, ys, lng, lnb):
    t = x1.shape[0]
    row = lambda i, *_: (i, 0)
    const = lambda i, *_: (0, 0)
    return pl.pallas_call(
        _combine_kernel,
        grid_spec=pltpu.PrefetchScalarGridSpec(
            num_scalar_prefetch=1,
            grid=(t // COMBINE_TM,),
            in_specs=[pl.BlockSpec((SUBLANES, COMBINE_TM), lambda i, *_: (0, i), memory_space=pltpu.SMEM),
                      pl.BlockSpec((COMBINE_TM, D_MODEL), row),
                      pl.BlockSpec((COMBINE_TM, ROUTE_LANES), row),
                      pl.BlockSpec(memory_space=pl.ANY),
                      pl.BlockSpec((1, D_MODEL), const),
                      pl.BlockSpec((1, D_MODEL), const)],
            out_specs=pl.BlockSpec((COMBINE_TM, D_MODEL), row),
            scratch_shapes=[pltpu.VMEM((2, COMBINE_TM, D_MODEL), jnp.float32), pltpu.SemaphoreType.DMA(())],
        ),
        out_shape=jax.ShapeDtypeStruct((t, D_MODEL), jnp.float32),
        compiler_params=_params(1),
        name="moe_combine",
    )(offs, route, x1, meta, ys, lng, lnb)


def _segment_layout(counts):
    padded = ((counts + FFN_TM - 1) // FFN_TM) * FFN_TM
    ends = jnp.cumsum(padded)
    offs = ends - padded
    return offs.astype(jnp.int32), ends.astype(jnp.int32)


def _layer(x, l, w_cat, attn_sink, g_mix, wout_bf, ln1_g, ln1_b, wr, br, w_gate, w_up, w_down, ln2_g, ln2_b):
    bt = x.shape[0]
    t = bt * SEQ
    oa, ob = _token_mixer_heads(x, w_cat, attn_sink[l])
    x1, meta, metat = _mix(x.reshape(t, D_MODEL), oa.reshape(t, A_WIDTH), ob.reshape(t, B_WIDTH),
                           g_mix[l][None], wout_bf, ln1_g[l][None], ln1_b[l][None], wr, br)
    route, counts = _plan(metat)
    offs, ends = _segment_layout(counts[:, 0])
    n_tiles = _sorted_rows(t) // FFN_TM
    tile_expert = jnp.minimum(
        jnp.searchsorted(ends, jnp.arange(n_tiles, dtype=jnp.int32) * FFN_TM, side="right"),
        N_EXPERTS - 1).astype(jnp.int32)
    nused = (ends[-1:] // FFN_TM).astype(jnp.int32)
    xs = _dispatch(offs, counts[:, 0], route, x1)
    ys = _ffn(tile_expert, nused, xs, w_gate, w_up, w_down, l)
    x2 = _combine(offs, route, x1, meta, ys, ln2_g[l][None], ln2_b[l][None])
    return x2.reshape(bt, SEQ, D_MODEL)


def kernel(x_prompt, x_sample, w_in, attn_sink, g_mix, w_out, ln1_g, ln1_b, w_coarse, b_coarse, w_fine, b_fine, w_gate, w_up, w_down, ln2_g, ln2_b):
    n_prompt = x_prompt.shape[0]
    x = jnp.concatenate([x_prompt, x_sample], axis=0)
    pad = ROUTE_LANES - N_GROUPS - N_EXPERTS
    for l in range(DEPTH):
        wr = jnp.concatenate([w_coarse[l], w_fine[l], jnp.zeros((D_MODEL, pad), jnp.float32)], axis=1)
        br = jnp.concatenate([b_coarse[l], b_fine[l], jnp.zeros((pad,), jnp.float32)])[None]
        x = _layer(x, l, _prep_w_in(w_in[l]), attn_sink, g_mix, w_out[l].astype(jnp.bfloat16), ln1_g, ln1_b,
                   wr, br, w_gate, w_up, w_down, ln2_g, ln2_b)
    return x[:n_prompt], x[n_prompt:]
```

```python
import jax
import jax.numpy as jnp
import numpy as np
from jax import lax
from jax.experimental import pallas as pl
from jax.experimental.pallas import tpu as pltpu

D_MODEL = 1024
SEQ = 2048
DEPTH = 2
HEAD_DIM = 64
A_HEADS = 8
A_KV_HEADS = 2
A_HALF_WINDOW = 128
B_HEADS = 8
B_DILATIONS = (1, 4, 16)
B_HALF = 64
A_WIDTH = A_HEADS * HEAD_DIM
AKV_WIDTH = A_KV_HEADS * HEAD_DIM
B_WIDTH = B_HEADS * HEAD_DIM
N_GROUPS = 4
EXPERTS_PER_GROUP = 8
N_EXPERTS = N_GROUPS * EXPERTS_PER_GROUP
D_EXPERT = D_MODEL // 2
DEEPNORM_ALPHA = (2 * DEPTH) ** 0.25
LN_EPS = 1e-5
MASK_DIST = 1e30

LANES = 128
SUBLANES = 8
Q_BLOCK = 128
VMEM_LIMIT = 56 * 1024 * 1024


def _alibi_slopes(n):
    return jnp.asarray(np.array([2.0 ** (-8.0 * (h + 1) / n) for h in range(n)], dtype=np.float32))


def _params(n_axes):
    return pltpu.CompilerParams(dimension_semantics=("arbitrary",) * n_axes, vmem_limit_bytes=VMEM_LIMIT)


PROJ_TM = 512
W_CAT_COLS = 2560


def _inproj_kernel(x_ref, w_ref, qa_ref, ka_ref, va_ref,
                   q1_ref, k1_ref, v1_ref, q4_ref, k4_ref, v4_ref, q16_ref, k16_ref, v16_ref,
                   acc_ref):
    xb = x_ref[...].astype(jnp.bfloat16)

    def proj(c0, width):
        return jnp.dot(xb, w_ref[:, c0:c0 + width], preferred_element_type=jnp.float32)

    qa_ref[...] = proj(0, 512).astype(jnp.bfloat16)
    ka_ref[...] = proj(512, 256).astype(jnp.bfloat16)
    va_ref[...] = proj(768, 256).astype(jnp.bfloat16)
    for j, (o1, o4, o16) in enumerate(((q1_ref, q4_ref, q16_ref), (k1_ref, k4_ref, k16_ref),
                                       (v1_ref, v4_ref, v16_ref))):
        acc = proj(1024 + 512 * j, 512)
        o1[0] = acc.astype(jnp.bfloat16)
        for c in range(512 // LANES):
            acc_ref[c] = acc[:, c * LANES:(c + 1) * LANES]
        for c in range(512 // LANES):
            lanes = slice(c * LANES, (c + 1) * LANES)
            for r in range(4):
                o4[r, :, lanes] = acc_ref[c, pl.ds(r, PROJ_TM // 4, stride=4), :].astype(jnp.bfloat16)
            for r in range(16):
                o16[r, :, lanes] = acc_ref[c, pl.ds(r, PROJ_TM // 16, stride=16), :].astype(jnp.bfloat16)


def _inproj(x, w_cat):
    bt = x.shape[0]
    nt = SEQ // PROJ_TM
    bf = jnp.bfloat16

    def dil_shape(d):
        return jax.ShapeDtypeStruct((bt, d, SEQ // d, B_WIDTH), bf)

    def dil_spec(d):
        return pl.BlockSpec((None, d, PROJ_TM // d, B_WIDTH), lambda b, i: (b, 0, i, 0))

    out_shape = [jax.ShapeDtypeStruct((bt, SEQ, 512), bf),
                 jax.ShapeDtypeStruct((bt, SEQ, 256), bf),
                 jax.ShapeDtypeStruct((bt, SEQ, 256), bf)]
    out_specs = [pl.BlockSpec((None, PROJ_TM, 512), lambda b, i: (b, i, 0)),
                 pl.BlockSpec((None, PROJ_TM, 256), lambda b, i: (b, i, 0)),
                 pl.BlockSpec((None, PROJ_TM, 256), lambda b, i: (b, i, 0))]
    for d in B_DILATIONS:
        out_shape += [dil_shape(d)] * 3
        out_specs += [dil_spec(d)] * 3
    return pl.pallas_call(
        _inproj_kernel,
        grid=(bt, nt),
        in_specs=[pl.BlockSpec((None, PROJ_TM, D_MODEL), lambda b, i: (b, i, 0)),
                  pl.BlockSpec((D_MODEL, W_CAT_COLS), lambda b, i: (0, 0))],
        out_specs=out_specs,
        out_shape=out_shape,
        scratch_shapes=[pltpu.VMEM((512 // LANES, PROJ_TM, LANES), jnp.float32)],
        compiler_params=_params(2),
        name="inproj",
    )(x, w_cat)


BLOCKS_PER_ITER = 2


def _dist_tables(nk, half):
    r = np.arange(Q_BLOCK)[:, None]
    c = np.arange(nk)[None, :]
    tabs = []
    for rel in (half, 0, nk - Q_BLOCK):
        d = np.abs(rel + r - c).astype(np.float32)
        tabs.append(np.where(d <= half, d, np.float32(MASK_DIST)))
    return np.stack(tabs).astype(np.float32)


def _fill_bias(bias_ref, dist_ref, slope_a, slope_b, nk):
    for g in range(dist_ref.shape[0]):
        d = dist_ref[g]
        bias_ref[g, :, 0:nk] = d * (-slope_a)
        bias_ref[g, :, nk:2 * nk] = d * (-slope_b)


def _geometry(blk, nblk):
    return jnp.where(blk == 0, 1, jnp.where(blk == nblk - 1, 2, 0))


def _band_block(q2, kw, vw, bias, nk, sink_a=None, sink_b=None):
    lane_k = lax.broadcasted_iota(jnp.int32, (nk, LANES), 1) < HEAD_DIM
    zero = jnp.zeros_like(kw)
    kbd = jnp.concatenate([jnp.where(lane_k, kw, zero), jnp.where(lane_k, zero, kw)], axis=0)
    vbd = jnp.concatenate([jnp.where(lane_k, vw, zero), jnp.where(lane_k, zero, vw)], axis=0)
    s = lax.dot_general(q2, kbd, (((1,), (1,)), ((), ())), preferred_element_type=jnp.float32) + bias
    s_a, s_b = s[:, :nk], s[:, nk:]
    m_a = jnp.max(s_a, axis=-1, keepdims=True)
    m_b = jnp.max(s_b, axis=-1, keepdims=True)
    if sink_a is not None:
        m_a = jnp.maximum(m_a, sink_a)
        m_b = jnp.maximum(m_b, sink_b)
    p_a = jnp.exp(s_a - m_a)
    p_b = jnp.exp(s_b - m_b)
    l_a = jnp.sum(p_a, axis=-1, keepdims=True)
    l_b = jnp.sum(p_b, axis=-1, keepdims=True)
    if sink_a is not None:
        l_a = l_a + jnp.exp(sink_a - m_a)
        l_b = l_b + jnp.exp(sink_b - m_b)
    p = jnp.concatenate([p_a, p_b], axis=1).astype(jnp.bfloat16)
    o = jnp.dot(p, vbd, preferred_element_type=jnp.float32)
    lane_o = lax.broadcasted_iota(jnp.int32, (Q_BLOCK, LANES), 1) < HEAD_DIM
    o = o / jnp.where(lane_o, l_a, l_b)
    lse = jnp.where(lane_o, m_a + jnp.log(l_a), m_b + jnp.log(l_b))
    return o, lse


def _attn_a_kernel(slopes_ref, sink_ref, dist_ref, q_ref, k_ref, v_ref, o_ref, bias_ref):
    hp = pl.program_id(1)
    sink_a = sink_ref[2 * hp]
    sink_b = sink_ref[2 * hp + 1]
    nk = 3 * A_HALF_WINDOW
    nblk = SEQ // Q_BLOCK
    _fill_bias(bias_ref, dist_ref, slopes_ref[2 * hp], slopes_ref[2 * hp + 1], nk)

    def body(i, carry):
        for u in range(BLOCKS_PER_ITER):
            blk = i * BLOCKS_PER_ITER + u
            q0 = pl.multiple_of(blk * Q_BLOCK, Q_BLOCK)
            ws = pl.multiple_of(jnp.clip(q0 - A_HALF_WINDOW, 0, SEQ - nk), Q_BLOCK)
            o, _ = _band_block(q_ref[pl.ds(q0, Q_BLOCK), :], k_ref[pl.ds(ws, nk), :], v_ref[pl.ds(ws, nk), :],
                               bias_ref[_geometry(blk, nblk)], nk, sink_a, sink_b)
            o_ref[pl.ds(q0, Q_BLOCK), :] = o
        return carry

    lax.fori_loop(0, nblk // BLOCKS_PER_ITER, body, 0)


def _attn_a(qa, ka2, va2, slopes, sink):
    bt = qa.shape[0]
    nk = 3 * A_HALF_WINDOW
    dist = jnp.asarray(_dist_tables(nk, A_HALF_WINDOW))
    return pl.pallas_call(
        _attn_a_kernel,
        grid_spec=pltpu.PrefetchScalarGridSpec(
            num_scalar_prefetch=2,
            grid=(bt, A_HEADS // 2),
            in_specs=[pl.BlockSpec((3, Q_BLOCK, nk), lambda b, h, *_: (0, 0, 0)),
                      pl.BlockSpec((None, SEQ, LANES), lambda b, h, *_: (b, 0, h)),
                      pl.BlockSpec((None, SEQ, LANES), lambda b, h, *_: (b, 0, h // 2)),
                      pl.BlockSpec((None, SEQ, LANES), lambda b, h, *_: (b, 0, h // 2))],
            out_specs=pl.BlockSpec((None, SEQ, LANES), lambda b, h, *_: (b, 0, h)),
            scratch_shapes=[pltpu.VMEM((3, Q_BLOCK, 2 * nk), jnp.float32)],
        ),
        out_shape=jax.ShapeDtypeStruct((bt, SEQ, A_WIDTH), jnp.float32),
        compiler_params=_params(2),
        name="attn_window",
    )(slopes, sink, dist, qa, ka2, va2)


def _attn_b_kernel(slopes_ref, dist_ref, dist16_ref,
                   q1_ref, k1_ref, v1_ref, q4_ref, k4_ref, v4_ref, q16_ref, k16_ref, v16_ref,
                   o_ref, bias1_ref, bias4_ref, bias16_ref, osc_ref, lsc_ref):
    hp = pl.program_id(1)
    refs = ((q1_ref, k1_ref, v1_ref, dist_ref, bias1_ref), (q4_ref, k4_ref, v4_ref, dist_ref, bias4_ref),
            (q16_ref, k16_ref, v16_ref, dist16_ref, bias16_ref))
    for p, dil in enumerate(B_DILATIONS):
        q_ref, k_ref, v_ref, d_ref, bias_ref = refs[p]
        length = SEQ // dil
        nblk = length // Q_BLOCK
        nk = min(4 * B_HALF, length)
        _fill_bias(bias_ref, d_ref, slopes_ref[2 * hp] * float(dil), slopes_ref[2 * hp + 1] * float(dil), nk)

        def body(i, carry, q_ref=q_ref, k_ref=k_ref, v_ref=v_ref, bias_ref=bias_ref, length=length,
                 nblk=nblk, nk=nk, dil=dil, p=p):
            for u in range(BLOCKS_PER_ITER):
                idx = i * BLOCKS_PER_ITER + u
                r = idx // nblk
                blk = idx % nblk
                q0 = pl.multiple_of(blk * Q_BLOCK, Q_BLOCK)
                ws = pl.multiple_of(jnp.clip(q0 - B_HALF, 0, length - nk), B_HALF)
                bias = bias_ref[0] if nblk == 1 else bias_ref[_geometry(blk, nblk)]
                o, lse = _band_block(q_ref[r, pl.ds(q0, Q_BLOCK), :], k_ref[r, pl.ds(ws, nk), :],
                                     v_ref[r, pl.ds(ws, nk), :], bias, nk)
                if dil == 1:
                    rows = pl.ds(q0, Q_BLOCK)
                else:
                    rows = pl.ds(q0 * dil + r, Q_BLOCK, stride=dil)
                osc_ref[p, rows, :] = o
                lsc_ref[p, rows, :] = lse
            return carry

        lax.fori_loop(0, SEQ // Q_BLOCK // BLOCKS_PER_ITER, body, 0)

    def combine(i, carry):
        rows = pl.ds(pl.multiple_of(i * Q_BLOCK, Q_BLOCK), Q_BLOCK)
        l0, l1, l2 = lsc_ref[0, rows, :], lsc_ref[1, rows, :], lsc_ref[2, rows, :]
        m = jnp.maximum(jnp.maximum(l0, l1), l2)
        e0, e1, e2 = jnp.exp(l0 - m), jnp.exp(l1 - m), jnp.exp(l2 - m)
        num = e0 * osc_ref[0, rows, :] + e1 * osc_ref[1, rows, :] + e2 * osc_ref[2, rows, :]
        o_ref[rows, :] = num / (e0 + e1 + e2)
        return carry

    lax.fori_loop(0, SEQ // Q_BLOCK, combine, 0)


def _attn_b(qkv_layouts, slopes):
    bt = qkv_layouts[0].shape[0]
    nk = 4 * B_HALF
    dist = jnp.asarray(_dist_tables(nk, B_HALF))
    dist16 = jnp.asarray(_dist_tables(Q_BLOCK, B_HALF)[1:2])
    in_specs = [pl.BlockSpec((3, Q_BLOCK, nk), lambda b, h, *_: (0, 0, 0)),
                pl.BlockSpec((1, Q_BLOCK, Q_BLOCK), lambda b, h, *_: (0, 0, 0))]
    for d in B_DILATIONS:
        in_specs += [pl.BlockSpec((None, d, SEQ // d, LANES), lambda b, h, *_: (b, 0, 0, h))] * 3
    return pl.pallas_call(
        _attn_b_kernel,
        grid_spec=pltpu.PrefetchScalarGridSpec(
            num_scalar_prefetch=1,
            grid=(bt, B_HEADS // 2),
            in_specs=in_specs,
            out_specs=pl.BlockSpec((None, SEQ, LANES), lambda b, h, *_: (b, 0, h)),
            scratch_shapes=[pltpu.VMEM((3, Q_BLOCK, 2 * nk), jnp.float32),
                            pltpu.VMEM((3, Q_BLOCK, 2 * nk), jnp.float32),
                            pltpu.VMEM((1, Q_BLOCK, 2 * Q_BLOCK), jnp.float32),
                            pltpu.VMEM((3, SEQ, LANES), jnp.float32),
                            pltpu.VMEM((3, SEQ, LANES), jnp.float32)],
        ),
        out_shape=jax.ShapeDtypeStruct((bt, SEQ, B_WIDTH), jnp.float32),
        compiler_params=_params(2),
        name="attn_dilated",
    )(slopes, dist, dist16, *qkv_layouts)


def _prep_w_in(w_in_l):
    scale = HEAD_DIM ** -0.5
    qa = w_in_l[:, :A_WIDTH] * scale
    ka = w_in_l[:, A_WIDTH:A_WIDTH + AKV_WIDTH]
    va = w_in_l[:, A_WIDTH + AKV_WIDTH:A_WIDTH + 2 * AKV_WIDTH]
    rest = w_in_l[:, A_WIDTH + 2 * AKV_WIDTH:]
    qb = rest[:, :B_WIDTH] * scale
    kvb = rest[:, B_WIDTH:]

    def dup(w):
        h0, h1 = w[:, :HEAD_DIM], w[:, HEAD_DIM:]
        return jnp.concatenate([h0, h0, h1, h1], axis=1)

    return jnp.concatenate([qa, dup(ka), dup(va), qb, kvb], axis=1).astype(jnp.bfloat16)


def _token_mixer_heads(x, w_cat, sink_l):
    outs = _inproj(x, w_cat)
    qa, ka2, va2 = outs[:3]
    oa = _attn_a(qa, ka2, va2, _alibi_slopes(A_HEADS), sink_l)
    ob = _attn_b(outs[3:], _alibi_slopes(B_HEADS))
    return oa, ob


MIX_TM = 512
ROUTE_LANES = LANES
FINE_LANE0 = N_GROUPS


def _layernorm(h, g, b):
    mu = jnp.mean(h, axis=-1, keepdims=True)
    c = h - mu
    var = jnp.mean(c * c, axis=-1, keepdims=True)
    return c * lax.rsqrt(var + LN_EPS) * g + b


def _rmsnorm(o, g):
    return o * lax.rsqrt(jnp.mean(o * o, axis=-1, keepdims=True) + LN_EPS) * g


def _mix_kernel(x_ref, oa_ref, ob_ref, gmix_ref, wout_ref, lng_ref, lnb_ref, wr_ref, br_ref,
                x1_ref, meta_ref, metat_ref):
    ya = _rmsnorm(oa_ref[...], gmix_ref[:, :A_WIDTH])
    yb = _rmsnorm(ob_ref[...], gmix_ref[:, A_WIDTH:])
    y = jnp.concatenate([ya, yb], axis=-1).astype(jnp.bfloat16)
    att = jnp.dot(y, wout_ref[...], preferred_element_type=jnp.float32)
    x1 = _layernorm(DEEPNORM_ALPHA * x_ref[...] + att, lng_ref[...], lnb_ref[...])
    x1_ref[...] = x1

    xh = x1.astype(jnp.bfloat16)
    xl = (x1 - xh.astype(jnp.float32)).astype(jnp.bfloat16)
    hi = jnp.dot(xh, wr_ref[...], preferred_element_type=jnp.float32)
    lo = jnp.dot(xl, wr_ref[:, :ROUTE_LANES], preferred_element_type=jnp.float32)
    logits = hi[:, :ROUTE_LANES] + (hi[:, ROUTE_LANES:] + lo) + br_ref[...]
    lane = lax.broadcasted_iota(jnp.int32, logits.shape, 1).astype(jnp.float32)
    neg = jnp.float32(-jnp.inf)
    big = jnp.float32(ROUTE_LANES)
    is_coarse = lane < N_GROUPS
    cl = jnp.where(is_coarse, logits, neg)
    cmax = jnp.max(cl, axis=-1, keepdims=True)
    g_idx = jnp.min(jnp.where(cl == cmax, lane, big), axis=-1, keepdims=True)
    g_w = 1.0 / jnp.sum(jnp.where(is_coarse, jnp.exp(logits - cmax), 0.0), axis=-1, keepdims=True)
    lo = FINE_LANE0 + EXPERTS_PER_GROUP * g_idx
    fl = jnp.where((lane >= lo) & (lane < lo + EXPERTS_PER_GROUP), logits, neg)
    v1 = jnp.max(fl, axis=-1, keepdims=True)
    i1 = jnp.min(jnp.where(fl == v1, lane, big), axis=-1, keepdims=True)
    fl2 = jnp.where(lane == i1, neg, fl)
    v2 = jnp.max(fl2, axis=-1, keepdims=True)
    i2 = jnp.min(jnp.where(fl2 == v2, lane, big), axis=-1, keepdims=True)
    t = jnp.exp(v2 - v1)
    w1 = g_w / (1.0 + t)
    w2 = g_w * t / (1.0 + t)
    meta = jnp.where(lane == 0, i1 - FINE_LANE0,
                     jnp.where(lane == 1, i2 - FINE_LANE0,
                               jnp.where(lane == 2, w1, jnp.where(lane == 3, w2, 0.0))))
    meta_ref[...] = meta
    metat_ref[...] = meta.T[:SUBLANES, :]


def _mix(x, oa, ob, gmix, wout_bf, lng, lnb, wr, br):
    t = x.shape[0]
    row = lambda i: (i, 0)
    const = lambda i: (0, 0)
    return pl.pallas_call(
        _mix_kernel,
        grid=(t // MIX_TM,),
        in_specs=[pl.BlockSpec((MIX_TM, D_MODEL), row),
                  pl.BlockSpec((MIX_TM, A_WIDTH), row),
                  pl.BlockSpec((MIX_TM, B_WIDTH), row),
                  pl.BlockSpec((1, D_MODEL), const),
                  pl.BlockSpec((D_MODEL, D_MODEL), const),
                  pl.BlockSpec((1, D_MODEL), const),
                  pl.BlockSpec((1, D_MODEL), const),
                  pl.BlockSpec((D_MODEL, 2 * ROUTE_LANES), const),
                  pl.BlockSpec((1, ROUTE_LANES), const)],
        out_specs=[pl.BlockSpec((MIX_TM, D_MODEL), row),
                   pl.BlockSpec((MIX_TM, ROUTE_LANES), row),
                   pl.BlockSpec((SUBLANES, MIX_TM), lambda i: (0, i))],
        out_shape=[jax.ShapeDtypeStruct((t, D_MODEL), jnp.float32),
                   jax.ShapeDtypeStruct((t, ROUTE_LANES), jnp.float32),
                   jax.ShapeDtypeStruct((SUBLANES, t), jnp.float32)],
        compiler_params=_params(1),
        name="mix_out_router",
    )(x, oa, ob, gmix, wout_bf, lng, lnb, wr, br)


PLAN_TM = 512
FFN_TM = 256


def _plan_kernel(metat_ref, tri_ref, tri_e_ref, pos_ref, counts_ref, ends_ref, carry_ref, offs_ref):
    ps = pl.program_id(0)
    i = pl.program_id(1)

    @pl.when((ps == 0) & (i == 0))
    def _():
        carry_ref[...] = jnp.zeros_like(carry_ref)

    @pl.when((ps == 1) & (i == 0))
    def _():
        counts = carry_ref[...]
        padded = jnp.floor((counts + (FFN_TM - 1)) * (1.0 / FFN_TM)) * FFN_TM
        p_hi = jnp.floor(padded * (1.0 / 65536.0))
        p_lo = padded - p_hi * 65536.0
        tri_e = tri_e_ref[...]
        offs = (jnp.dot(tri_e, (p_hi * (1.0 / FFN_TM)).astype(jnp.bfloat16),
                        preferred_element_type=jnp.float32) * (65536.0 * FFN_TM)
                + jnp.dot(tri_e, (p_lo * (1.0 / FFN_TM)).astype(jnp.bfloat16),
                          preferred_element_type=jnp.float32) * FFN_TM)
        offs_ref[...] = offs
        counts_ref[...] = counts.astype(jnp.int32)
        ends_ref[...] = (offs + padded).astype(jnp.int32)
        carry_ref[...] = jnp.zeros_like(carry_ref)

    e1 = metat_ref[0:1, :]
    e2 = metat_ref[1:2, :]
    eid = lax.broadcasted_iota(jnp.int32, (N_EXPERTS, PLAN_TM), 0).astype(jnp.float32)
    oh1 = eid == e1
    oh2 = eid == e2
    cnt = jnp.where(oh1, 1.0, jnp.where(oh2, 1.0, 0.0))

    @pl.when(ps == 0)
    def _():
        pos_ref[...] = jnp.zeros_like(pos_ref)

    @pl.when(ps == 1)
    def _():
        before = jnp.dot(cnt.astype(jnp.bfloat16), tri_ref[...], preferred_element_type=jnp.float32)
        slot = before + (carry_ref[:, 0:1] + offs_ref[:, 0:1])
        pos1 = jnp.sum(jnp.where(oh1, slot, 0.0), axis=0, keepdims=True)
        pos2 = jnp.sum(jnp.where(oh2, slot, 0.0), axis=0, keepdims=True)
        sub = lax.broadcasted_iota(jnp.int32, (SUBLANES, PLAN_TM), 0)
        pos_ref[...] = jnp.where(sub == 0, pos1, jnp.where(sub == 1, pos2, 0.0)).astype(jnp.int32)

    carry_ref[...] = carry_ref[...] + jnp.sum(cnt, axis=1, keepdims=True)


def _plan(metat):
    t = metat.shape[1]
    tri = jnp.asarray(np.triu(np.ones((PLAN_TM, PLAN_TM), np.float32), k=1), jnp.bfloat16)
    tri_e = jnp.asarray(np.tril(np.ones((N_EXPERTS, N_EXPERTS), np.float32), k=-1), jnp.bfloat16)
    return pl.pallas_call(
        _plan_kernel,
        grid=(2, t // PLAN_TM),
        in_specs=[pl.BlockSpec((SUBLANES, PLAN_TM), lambda ps, i: (0, i)),
                  pl.BlockSpec((PLAN_TM, PLAN_TM), lambda ps, i: (0, 0)),
                  pl.BlockSpec((N_EXPERTS, N_EXPERTS), lambda ps, i: (0, 0))],
        out_specs=[pl.BlockSpec((SUBLANES, PLAN_TM), lambda ps, i: (0, i * ps)),
                   pl.BlockSpec((N_EXPERTS, LANES), lambda ps, i: (0, 0)),
                   pl.BlockSpec((N_EXPERTS, LANES), lambda ps, i: (0, 0))],
        out_shape=[jax.ShapeDtypeStruct((SUBLANES, t), jnp.int32),
                   jax.ShapeDtypeStruct((N_EXPERTS, LANES), jnp.int32),
                   jax.ShapeDtypeStruct((N_EXPERTS, LANES), jnp.int32)],
        scratch_shapes=[pltpu.VMEM((N_EXPERTS, LANES), jnp.float32),
                        pltpu.VMEM((N_EXPERTS, LANES), jnp.float32)],
        compiler_params=_params(2),
        name="route_plan",
    )(metat, tri, tri_e)


DISPATCH_TM = 1024
ISSUE_UNROLL = 8
TILE_SHIFT = FFN_TM.bit_length() - 1


def _sorted_rows(t):
    return 2 * t + N_EXPERTS * FFN_TM


def _dispatch_kernel(counts_ref, ends_ref, pos_ref, x_ref, xs_ref, zero_ref, sem, zsem):
    @pl.when(pl.program_id(0) == 0)
    def _():
        zero_ref[...] = jnp.zeros_like(zero_ref)
        n_tiles = xs_ref.shape[0] // FFN_TM
        nused = lax.shift_right_logical(ends_ref[N_EXPERTS - 1], TILE_SHIFT)

        def zero_tile(j):
            return pltpu.make_async_copy(
                zero_ref, xs_ref.at[pl.ds(pl.multiple_of(j * FFN_TM, FFN_TM), FFN_TM)], zsem)

        def last_tile(e, fn):
            @pl.when((counts_ref[e] & (FFN_TM - 1)) != 0)
            def _():
                fn(zero_tile(lax.shift_right_logical(ends_ref[e], TILE_SHIFT) - 1))

        def start(e, c):
            last_tile(e, lambda cp: cp.start())
            return c

        def wait(e, c):
            last_tile(e, lambda cp: cp.wait())
            return c

        lax.fori_loop(0, N_EXPERTS, start, 0)
        lax.fori_loop(nused, n_tiles, lambda j, c: (zero_tile(j).start(), c)[1], 0)
        lax.fori_loop(0, N_EXPERTS, wait, 0)
        lax.fori_loop(nused, n_tiles, lambda j, c: (zero_tile(j).wait(), c)[1], 0)

    def issue(j, carry):
        base = pl.multiple_of(j * ISSUE_UNROLL, ISSUE_UNROLL)
        for u in range(ISSUE_UNROLL):
            src = x_ref.at[pl.ds(base + u, 1)]
            for k in range(2):
                pltpu.make_async_copy(src, xs_ref.at[pos_ref[k, base + u]], sem).start()
        return carry

    lax.fori_loop(0, DISPATCH_TM // ISSUE_UNROLL, issue, 0)
    for _ in range(2):
        rows = xs_ref.at[pl.ds(0, DISPATCH_TM)]
        pltpu.make_async_copy(rows, rows, sem).wait()


def _dispatch(counts, ends, pos, x1):
    t = x1.shape[0]
    return pl.pallas_call(
        _dispatch_kernel,
        grid_spec=pltpu.PrefetchScalarGridSpec(
            num_scalar_prefetch=2,
            grid=(t // DISPATCH_TM,),
            in_specs=[pl.BlockSpec((SUBLANES, DISPATCH_TM), lambda i, *_: (0, i), memory_space=pltpu.SMEM),
                      pl.BlockSpec((DISPATCH_TM, D_MODEL), lambda i, *_: (i, 0))],
            out_specs=pl.BlockSpec(memory_space=pl.ANY),
            scratch_shapes=[pltpu.VMEM((FFN_TM, 1, D_MODEL), jnp.float32),
                            pltpu.SemaphoreType.DMA(()), pltpu.SemaphoreType.DMA(())],
        ),
        out_shape=jax.ShapeDtypeStruct((_sorted_rows(t), 1, D_MODEL), jnp.float32),
        compiler_params=_params(1),
        name="moe_dispatch",
    )(counts, ends, pos, x1)


def _ffn_kernel(te_ref, nused_ref, xs_ref, wg_ref, wu_ref, wd_ref, ys_ref, x2d_ref):
    used = pl.program_id(0) < nused_ref[0]

    @pl.when(used)
    def _():
        x2d_ref[...] = xs_ref[...].reshape(FFN_TM, D_MODEL)
        x = x2d_ref[...].astype(jnp.bfloat16)
        g = jnp.dot(x, wg_ref[...].astype(jnp.bfloat16), preferred_element_type=jnp.float32)
        u = jnp.dot(x, wu_ref[...].astype(jnp.bfloat16), preferred_element_type=jnp.float32)
        h = (g * jax.nn.sigmoid(g) * u).astype(jnp.bfloat16)
        ys_ref[:, 0, :] = jnp.dot(h, wd_ref[...].astype(jnp.bfloat16), preferred_element_type=jnp.float32)

    @pl.when(jnp.logical_not(used))
    def _():
        ys_ref[...] = jnp.zeros_like(ys_ref)


def _ffn(tile_expert, nused, xs, w_gate, w_up, w_down, layer):
    rows = xs.shape[0]
    in_tile = lambda j, te, nu: (jnp.minimum(j, nu[0] - 1), 0, 0)
    out_tile = lambda j, te, nu: (j, 0, 0)
    wspec = lambda shape: pl.BlockSpec((None, None) + shape, lambda j, te, nu: (layer, te[j], 0, 0))
    return pl.pallas_call(
        _ffn_kernel,
        grid_spec=pltpu.PrefetchScalarGridSpec(
            num_scalar_prefetch=2,
            grid=(rows // FFN_TM,),
            in_specs=[pl.BlockSpec((FFN_TM, 1, D_MODEL), in_tile),
                      wspec((D_MODEL, D_EXPERT)), wspec((D_MODEL, D_EXPERT)), wspec((D_EXPERT, D_MODEL))],
            out_specs=pl.BlockSpec((FFN_TM, 1, D_MODEL), out_tile),
            scratch_shapes=[pltpu.VMEM((FFN_TM, D_MODEL), jnp.float32)],
        ),
        out_shape=jax.ShapeDtypeStruct((rows, 1, D_MODEL), jnp.float32),
        compiler_params=_params(1),
        name="moe_ffn",
    )(tile_expert, nused, xs, w_gate, w_up, w_down)


COMBINE_TM = 256


def _combine_kernel(pos_ref, posn_ref, x1_ref, meta_ref, ys_ref, lng_ref, lnb_ref, o_ref, ybuf_ref, sem):
    i = pl.program_id(0)
    n = pl.num_programs(0)

    def gather(p_ref, slot):
        def issue(j, carry):
            base = pl.multiple_of(j * ISSUE_UNROLL, ISSUE_UNROLL)
            for u in range(ISSUE_UNROLL):
                for k in range(2):
                    pltpu.make_async_copy(ys_ref.at[p_ref[k, base + u]],
                                          ybuf_ref.at[slot, k, pl.ds(base + u, 1)], sem.at[slot]).start()
            return carry

        lax.fori_loop(0, COMBINE_TM // ISSUE_UNROLL, issue, 0)

    @pl.when(i == 0)
    def _():
        gather(pos_ref, 0)

    @pl.when(i + 1 < n)
    def _():
        gather(posn_ref, (i + 1) % 2)

    slot = i % 2
    for k in range(2):
        rows = ys_ref.at[pl.ds(0, COMBINE_TM)]
        pltpu.make_async_copy(rows, rows, sem.at[slot]).wait()
    meta = meta_ref[...]
    moe = meta[:, 2:3] * ybuf_ref[slot, 0] + meta[:, 3:4] * ybuf_ref[slot, 1]
    o_ref[...] = _layernorm(DEEPNORM_ALPHA * x1_ref[...] + moe, lng_ref[...], lnb_ref[...])


def _combine(pos, x1, meta, ys, lng, lnb):
    t = x1.shape[0]
    n = t // COMBINE_TM
    row = lambda i: (i, 0)
    const = lambda i: (0, 0)
    return pl.pallas_call(
        _combine_kernel,
        grid=(n,),
        in_specs=[pl.BlockSpec((SUBLANES, COMBINE_TM), lambda i: (0, i), memory_space=pltpu.SMEM),
                  pl.BlockSpec((SUBLANES, COMBINE_TM), lambda i: (0, jnp.minimum(i + 1, n - 1)),
                               memory_space=pltpu.SMEM),
                  pl.BlockSpec((COMBINE_TM, D_MODEL), row),
                  pl.BlockSpec((COMBINE_TM, ROUTE_LANES), row),
                  pl.BlockSpec(memory_space=pl.ANY),
                  pl.BlockSpec((1, D_MODEL), const),
                  pl.BlockSpec((1, D_MODEL), const)],
        out_specs=pl.BlockSpec((COMBINE_TM, D_MODEL), row),
        scratch_shapes=[pltpu.VMEM((2, 2, COMBINE_TM, D_MODEL), jnp.float32), pltpu.SemaphoreType.DMA((2,))],
        out_shape=jax.ShapeDtypeStruct((t, D_MODEL), jnp.float32),
        compiler_params=_params(1),
        name="moe_combine",
    )(pos, pos, x1, meta, ys, lng, lnb)


def _split_hi_lo(w):
    hi = w.astype(jnp.bfloat16)
    lo = (w - hi.astype(jnp.float32)).astype(jnp.bfloat16)
    return jnp.concatenate([hi, lo], axis=1)


def _layer(x, l, w_cat, attn_sink, g_mix, wout_bf, ln1_g, ln1_b, wr, br, w_gate, w_up, w_down, ln2_g, ln2_b):
    bt = x.shape[0]
    t = bt * SEQ
    oa, ob = _token_mixer_heads(x, w_cat, attn_sink[l])
    x1, meta, metat = _mix(x.reshape(t, D_MODEL), oa.reshape(t, A_WIDTH), ob.reshape(t, B_WIDTH),
                           g_mix[l][None], wout_bf, ln1_g[l][None], ln1_b[l][None], wr, br)
    pos, counts, ends = _plan(metat)
    counts, ends = counts[:, 0], ends[:, 0]
    n_tiles = _sorted_rows(t) // FFN_TM
    tile_start = jnp.arange(n_tiles, dtype=jnp.int32) * FFN_TM
    tile_expert = jnp.minimum(jnp.sum((ends[None, :] <= tile_start[:, None]).astype(jnp.int32), axis=1),
                              N_EXPERTS - 1)
    nused = lax.shift_right_logical(ends[-1:], TILE_SHIFT)
    xs = _dispatch(counts, ends, pos, x1)
    ys = _ffn(tile_expert, nused, xs, w_gate, w_up, w_down, l)
    x2 = _combine(pos, x1, meta, ys, ln2_g[l][None], ln2_b[l][None])
    return x2.reshape(bt, SEQ, D_MODEL)


def kernel(x_prompt, x_sample, w_in, attn_sink, g_mix, w_out, ln1_g, ln1_b, w_coarse, b_coarse, w_fine, b_fine, w_gate, w_up, w_down, ln2_g, ln2_b):
    n_prompt = x_prompt.shape[0]
    x = jnp.concatenate([x_prompt, x_sample], axis=0)
    pad = ROUTE_LANES - N_GROUPS - N_EXPERTS
    for l in range(DEPTH):
        wr = _split_hi_lo(jnp.concatenate([w_coarse[l], w_fine[l], jnp.zeros((D_MODEL, pad), jnp.float32)], axis=1))
        br = jnp.concatenate([b_coarse[l], b_fine[l], jnp.zeros((pad,), jnp.float32)])[None]
        x = _layer(x, l, _prep_w_in(w_in[l]), attn_sink, g_mix, w_out[l].astype(jnp.bfloat16), ln1_g, ln1_b,
                   wr, br, w_gate, w_up, w_down, ln2_g, ln2_b)
    return x[:n_prompt], x[n_prompt:]
```
